```python
import math
import jax, jax.numpy as jnp
from jax import lax
import numpy as np

D_MODEL = 1024
BATCH = 8
SEQ = 2048
DEPTH = 2
DEC_BATCH = 128
DEC_SEQ = 8
PAST_LEN = 16384
PAGE_SIZE = 128

W_A = D_MODEL
H_A = 16
HD_A = W_A // H_A
CONV_A = 4
LRU_C = 8.0
W_B = D_MODEL
G_B = 16
GD_B = W_B // G_B
CHUNK = 128
W_C = D_MODEL
CONV_C = 3
D_FF = 4 * D_MODEL
N_MOD = 6
IN_SPLITS = (W_A, W_B, W_B, W_C, W_C, W_C, D_MODEL, D_MODEL, D_MODEL)
IN_COLS = sum(IN_SPLITS)
EPS = 1e-6

kernel_name = "hybrid_rglru_chunkmlp_shortconv_decoder_step"


def rmsnorm(x, g):
    xf = x.astype(jnp.float32)
    y = xf * lax.rsqrt(jnp.mean(xf * xf, axis=-1, keepdims=True) + EPS)
    return (y * g.astype(jnp.float32)).astype(x.dtype)


def layernorm(x, g, b):
    xf = x.astype(jnp.float32)
    mu = jnp.mean(xf, axis=-1, keepdims=True)
    var = jnp.mean(jnp.square(xf - mu), axis=-1, keepdims=True)
    y = (xf - mu) * lax.rsqrt(var + EPS)
    return (y * g.astype(jnp.float32) + b.astype(jnp.float32)).astype(x.dtype)


def causal_conv(x, prev, w, b=None):
    K = w.shape[0]
    T = x.shape[1]
    xp = jnp.concatenate([prev.astype(x.dtype), x], axis=1)
    y = xp[:, 0:T] * w[0]
    for k in range(1, K):
        y = y + xp[:, k:k + T] * w[k]
    if b is not None:
        y = y + b
    return y, xp[:, -(K - 1):]


def rg_lru(xc, h0, wa, ba, wx, bx, lam):
    B, T, W = xc.shape
    xh = xc.reshape(B, T, H_A, HD_A)
    gate_r = jax.nn.sigmoid(jnp.einsum('bthi,hij->bthj', xh, wa).reshape(B, T, W) + ba)
    gate_i = jax.nn.sigmoid(jnp.einsum('bthi,hij->bthj', xh, wx).reshape(B, T, W) + bx)
    log_a = -LRU_C * gate_r.astype(jnp.float32) * jax.nn.softplus(-lam.astype(jnp.float32))
    a = jnp.exp(log_a)
    mult = jnp.sqrt(-jnp.expm1(2.0 * log_a))
    bterm = mult * (gate_i * xc).astype(jnp.float32)
    bterm = bterm.at[:, 0].add(a[:, 0] * h0.astype(jnp.float32))

    def combine(left, right):
        al, bl = left
        ar, br = right
        return al * ar, ar * bl + br

    _, hs = lax.associative_scan(combine, (a, bterm), axis=1)
    return hs.astype(xc.dtype), hs[:, -1]


def chunk_mix(v, ws, bs):
    B, T, W = v.shape
    n_chunks = -(-T // CHUNK)
    t_pad = n_chunks * CHUNK
    vp = jnp.pad(v, ((0, 0), (0, t_pad - T), (0, 0)))
    vr = vp.reshape(B, n_chunks, CHUNK, G_B, GD_B)
    mask = jnp.tril(jnp.ones((CHUNK, CHUNK), dtype=bool))
    wm = jnp.where(mask[None], ws, jnp.zeros((), ws.dtype))
    out = jnp.einsum('gts,bnsgc->bntgc', wm, vr) + jnp.transpose(bs)[None, None, :, :, None]
    return out.reshape(B, t_pad, W)[:, :T]


def trunk_layer(x, c, h0, lconv0, sconv0, w_ada, b_ada, g_norm1, g_norm2, w_in,
                lru_conv_w, lru_conv_b, lru_wa, lru_ba, lru_wx, lru_bx, lru_lambda,
                cm_ln_g, cm_ln_b, cm_ws, cm_bs, sc_conv_w,
                w_br_a, w_br_b, w_br_c, w_o, w_ff1, w_ff2):
    mod = jax.nn.silu(c) @ w_ada + b_ada
    shift1, scale1, gate1, shift2, scale2, gate2 = [m[:, None, :] for m in jnp.split(mod, N_MOD, axis=-1)]

    h = rmsnorm(x, g_norm1) * (1.0 + scale1) + shift1
    z = h @ w_in
    idx = np.cumsum(IN_SPLITS)[:-1].tolist()
    xa, u, v, gb, gc, xc, g_a, g_b, g_c = jnp.split(z, idx, axis=-1)

    xa_c, lconv_new = causal_conv(xa, lconv0, lru_conv_w, lru_conv_b)
    ya, h_last = rg_lru(xa_c, h0, lru_wa, lru_ba, lru_wx, lru_bx, lru_lambda)

    vn = layernorm(v, cm_ln_g, cm_ln_b)
    yb = u * chunk_mix(vn, cm_ws, cm_bs)

    q = gc * xc
    qc, sconv_new = causal_conv(q, sconv0, sc_conv_w)
    yc = gb * qc

    merged = (jax.nn.sigmoid(g_a) * (ya @ w_br_a)
              + jax.nn.sigmoid(g_b) * (yb @ w_br_b)
              + jax.nn.sigmoid(g_c) * (yc @ w_br_c))
    x = x + gate1 * (merged @ w_o)

    h2 = rmsnorm(x, g_norm2) * (1.0 + scale2) + shift2
    ff = jnp.square(jax.nn.relu(h2 @ w_ff1)) @ w_ff2
    x = x + gate2 * ff
    return x, h_last, lconv_new, sconv_new, vn


def setup_inputs(seed: int = 0) -> dict:
    key = jax.random.key(seed)
    ks = iter(jax.random.split(key, 40))
    f32 = jnp.float32

    def nrm(shape, scale):
        return jax.random.normal(next(ks), shape, f32) * scale

    a_init = jax.random.uniform(next(ks), (DEPTH, W_A), f32, 0.9, 0.999)
    inp = {
        "x_prompt": nrm((BATCH, SEQ, D_MODEL), 1.0),
        "x_sample": nrm((DEC_BATCH, DEC_SEQ, D_MODEL), 1.0),
        "c_prompt": nrm((BATCH, D_MODEL), 1.0),
        "c_sample": nrm((DEC_BATCH, D_MODEL), 1.0),
        "state_lru_h": nrm((DEPTH, DEC_BATCH, W_A), 0.5),
        "state_lru_conv": nrm((DEPTH, DEC_BATCH, CONV_A - 1, W_A), 1.0),
        "state_sconv": nrm((DEPTH, DEC_BATCH, CONV_C - 1, W_C), 1.0),
        "w_ada": nrm((DEPTH, D_MODEL, N_MOD * D_MODEL), D_MODEL ** -0.5),
        "b_ada": nrm((DEPTH, N_MOD * D_MODEL), 0.01),
        "g_norm1": 1.0 + nrm((DEPTH, D_MODEL), 0.02),
        "g_norm2": 1.0 + nrm((DEPTH, D_MODEL), 0.02),
        "g_final": 1.0 + nrm((D_MODEL,), 0.02),
        "w_in": nrm((DEPTH, D_MODEL, IN_COLS), D_MODEL ** -0.5),
        "lru_conv_w": nrm((DEPTH, CONV_A, W_A), CONV_A ** -0.5),
        "lru_conv_b": nrm((DEPTH, W_A), 0.01),
        "lru_wa": nrm((DEPTH, H_A, HD_A, HD_A), HD_A ** -0.5),
        "lru_ba": nrm((DEPTH, W_A), 0.01),
        "lru_wx": nrm((DEPTH, H_A, HD_A, HD_A), HD_A ** -0.5),
        "lru_bx": nrm((DEPTH, W_A), 0.01),
        "lru_lambda": jnp.log(a_init) - jnp.log1p(-a_init),
        "cm_ln_g": 1.0 + nrm((DEPTH, W_B), 0.02),
        "cm_ln_b": nrm((DEPTH, W_B), 0.01),
        "cm_ws": nrm((DEPTH, G_B, CHUNK, CHUNK), CHUNK ** -0.5),
        "cm_bs": 1.0 + nrm((DEPTH, G_B, CHUNK), 0.01),
        "sc_conv_w": nrm((DEPTH, CONV_C, W_C), CONV_C ** -0.5),
        "w_br_a": nrm((DEPTH, W_A, D_MODEL), W_A ** -0.5),
        "w_br_b": nrm((DEPTH, W_B, D_MODEL), W_B ** -0.5),
        "w_br_c": nrm((DEPTH, W_C, D_MODEL), W_C ** -0.5),
        "w_o": nrm((DEPTH, D_MODEL, D_MODEL), D_MODEL ** -0.5),
        "w_ff1": nrm((DEPTH, D_MODEL, D_FF), D_MODEL ** -0.5),
        "w_ff2": nrm((DEPTH, D_FF, D_MODEL), D_FF ** -0.5),
    }
    return inp


def reference(x_prompt, x_sample, c_prompt, c_sample, state_lru_h, state_lru_conv, state_sconv,
              w_ada, b_ada, g_norm1, g_norm2, g_final, w_in, lru_conv_w, lru_conv_b,
              lru_wa, lru_ba, lru_wx, lru_bx, lru_lambda, cm_ln_g, cm_ln_b, cm_ws, cm_bs,
              sc_conv_w, w_br_a, w_br_b, w_br_c, w_o, w_ff1, w_ff2):
    xp, xs = x_prompt, x_sample
    bp = x_prompt.shape[0]
    hp_l, lcp_l, scp_l = [], [], []
    hs_l, lcs_l, scs_l, vs_l = [], [], [], []
    for l in range(DEPTH):
        weights = (w_ada[l], b_ada[l], g_norm1[l], g_norm2[l], w_in[l],
                   lru_conv_w[l], lru_conv_b[l], lru_wa[l], lru_ba[l], lru_wx[l], lru_bx[l],
                   lru_lambda[l], cm_ln_g[l], cm_ln_b[l], cm_ws[l], cm_bs[l], sc_conv_w[l],
                   w_br_a[l], w_br_b[l], w_br_c[l], w_o[l], w_ff1[l], w_ff2[l])
        h0p = jnp.zeros((bp, W_A), jnp.float32)
        lc0p = jnp.zeros((bp, CONV_A - 1, W_A), xp.dtype)
        sc0p = jnp.zeros((bp, CONV_C - 1, W_C), xp.dtype)
        xp, hp, lcp, scp, _vp = trunk_layer(xp, c_prompt, h0p, lc0p, sc0p, *weights)
        hp_l.append(hp); lcp_l.append(lcp); scp_l.append(scp)
        xs, hs, lcs, scs, vs = trunk_layer(xs, c_sample, state_lru_h[l], state_lru_conv[l],
                                           state_sconv[l], *weights)
        hs_l.append(hs); lcs_l.append(lcs); scs_l.append(scs); vs_l.append(vs)
    y_prompt = rmsnorm(xp, g_final)
    y_sample = rmsnorm(xs, g_final)
    return (y_prompt, y_sample,
            jnp.stack(hp_l), jnp.stack(lcp_l), jnp.stack(scp_l),
            jnp.stack(hs_l), jnp.stack(lcs_l), jnp.stack(scs_l), jnp.stack(vs_l))
```

```python
import functools

import jax
import jax.numpy as jnp
from jax import lax
from jax.experimental import pallas as pl
from jax.experimental.pallas import tpu as pltpu

D = 1024
D_FF = 4 * D
N_MOD = 6
EPS = 1e-6
LRU_C = 8.0
CHUNK = 128
N_GROUPS = 16
GROUP_W = D // N_GROUPS
SUBLANES = 8
LANES = 128
GATE_TILE = 256
ROWS = 512
SAMPLE_MIXER_ROWS = 256
VMEM_LIMIT = 60 * 1024 * 1024

F32 = jnp.float32
BF16 = jnp.bfloat16


def _sigmoid(x):
    return 1.0 / (1.0 + jnp.exp(-x))


def _dot(a, b):
    return jnp.dot(a, b, preferred_element_type=F32)


def _rms(x3):
    ms = jnp.mean(x3 * x3, axis=-1, keepdims=True)
    return x3 * lax.rsqrt(ms + EPS)


def _mod_kernel(c_ref, w_ref, b_ref, o_ref):
    c = c_ref[...]
    s = (c * _sigmoid(c)).astype(BF16)
    o_ref[0] = _dot(s, w_ref[0].astype(BF16)) + b_ref[0]


def _modulation(c_all, w_ada, b_ada):
    depth = w_ada.shape[0]
    n = c_all.shape[0]
    return pl.pallas_call(
        _mod_kernel,
        grid=(depth, N_MOD),
        in_specs=[
            pl.BlockSpec((n, D), lambda l, j: (0, 0)),
            pl.BlockSpec((1, D, D), lambda l, j: (l, 0, j)),
            pl.BlockSpec((1, 1, D), lambda l, j: (l, 0, j)),
        ],
        out_specs=pl.BlockSpec((1, n, D), lambda l, j: (l, 0, j)),
        out_shape=jax.ShapeDtypeStruct((depth, n, N_MOD * D), F32),
        compiler_params=pltpu.CompilerParams(
            dimension_semantics=("arbitrary", "arbitrary"), vmem_limit_bytes=VMEM_LIMIT),
        name="modulation",
    )(c_all, w_ada, b_ada.reshape(depth, 1, N_MOD * D))


def _group_scan(a, b, row8):
    for s in (1, 2, 4):
        keep = row8 >= s
        a_sh = jnp.where(keep, pltpu.roll(a, s, 0), 1.0)
        b_sh = jnp.where(keep, pltpu.roll(b, s, 0), 0.0)
        b = a * b_sh + b
        a = a * a_sh
    return a, b


def _mixer_kernel(prompt, nb, r, *refs):
    R = nb * r
    if prompt:
        (x_ref, mod_ref, g1_ref, win_ref, cw_ref, cb_ref, wg_ref, ba_ref, bx_ref, lam_ref,
         lng_ref, lnb_ref, cmw_ref, cmb_ref, sw_ref, wbr_ref, wo_ref,
         x1_ref, hs_ref, xa_ref, q_ref, xa_c, q_c, h_c) = refs
    else:
        (x_ref, mod_ref, h0_ref, lc_ref, sc_ref, g1_ref, win_ref, cw_ref, cb_ref, wg_ref, ba_ref,
         bx_ref, lam_ref, lng_ref, lnb_ref, wd_ref, cmb_ref, sw_ref, wbr_ref, wo_ref,
         x1_ref, hs_ref, xa_ref, q_ref, vn_ref) = refs

    if prompt:
        @pl.when(pl.program_id(1) == 0)
        def _():
            xa_c[...] = jnp.zeros_like(xa_c)
            q_c[...] = jnp.zeros_like(q_c)
            h_c[...] = jnp.zeros_like(h_c)

    x3 = x_ref[...]
    m = mod_ref[...]
    shift1, scale1, gate1 = m[:, :, 0:D], m[:, :, D:2 * D], m[:, :, 2 * D:3 * D]
    h3 = (_rms(x3) * g1_ref[...]) * (1.0 + scale1) + shift1
    h = h3.reshape(R, D).astype(BF16)

    def zcol(k):
        return _dot(h, win_ref[:, k * D:(k + 1) * D])

    row8 = lax.broadcasted_iota(jnp.int32, (R, D), 0) % SUBLANES

    def shifter(val, carry_ref, state_ref):
        if prompt:
            ext = jnp.concatenate([carry_ref[...], val], axis=0)
            carry_ref[...] = val[R - SUBLANES:]
            return lambda j: pltpu.roll(ext, j, 0)[SUBLANES:]
        past = state_ref[...].reshape(R, D)
        return lambda j: jnp.where(row8 >= j, pltpu.roll(val, j, 0),
                                   pltpu.roll(past, R - SUBLANES + j, 0))

    xa = zcol(0)
    xa_d = shifter(xa, xa_c if prompt else None, None if prompt else lc_ref)
    cw = cw_ref[...]
    xc = xa * cw[3:4] + xa_d(1) * cw[2:3] + xa_d(2) * cw[1:2] + xa_d(3) * cw[0:1] + cb_ref[...]
    xcb = xc.astype(BF16)
    pre_r, pre_i = [], []
    for j in range(D // GATE_TILE):
        g = _dot(xcb[:, j * GATE_TILE:(j + 1) * GATE_TILE], wg_ref[j])
        pre_r.append(g[:, :GATE_TILE])
        pre_i.append(g[:, GATE_TILE:])
    gate_r = _sigmoid(jnp.concatenate(pre_r, axis=1) + ba_ref[...])
    gate_i = _sigmoid(jnp.concatenate(pre_i, axis=1) + bx_ref[...])
    nl = -lam_ref[...]
    softplus = jnp.maximum(nl, 0.0) + jnp.log1p(jnp.exp(-jnp.abs(nl)))
    a = jnp.exp((-LRU_C * softplus) * gate_r)
    bt = jnp.sqrt(1.0 - a * a) * (gate_i * xc)
    A, B = _group_scan(a, bt, row8)
    if prompt:
        carry = h_c[...]
        pieces = []
        for i in range(R // SUBLANES):
            sl = slice(i * SUBLANES, (i + 1) * SUBLANES)
            hi = A[sl] * carry + B[sl]
            pieces.append(hi)
            carry = jnp.broadcast_to(hi[SUBLANES - 1:SUBLANES], (SUBLANES, D))
        hs = jnp.concatenate(pieces, axis=0)
        h_c[...] = carry
    else:
        h0 = jnp.broadcast_to(h0_ref[...], (nb, r, D)).reshape(R, D)
        hs = A * h0 + B
    ya = hs.astype(BF16)

    u = zcol(1)
    v = zcol(2)
    mu = jnp.mean(v, axis=-1, keepdims=True)
    vc = v - mu
    var = jnp.mean(vc * vc, axis=-1, keepdims=True)
    vn = vc * lax.rsqrt(var + EPS) * lng_ref[...] + lnb_ref[...]
    if prompt:
        vnb = vn.astype(BF16)
        ti = lax.broadcasted_iota(jnp.int32, (CHUNK, CHUNK), 0)
        si = lax.broadcasted_iota(jnp.int32, (CHUNK, CHUNK), 1)
        low_lanes = lax.broadcasted_iota(jnp.int32, (CHUNK, LANES), 1) < GROUP_W
        stacked = []
        for p in range(D // LANES):
            w0 = jnp.where(ti >= si, cmw_ref[2 * p], 0.0).astype(BF16)
            w1 = jnp.where(ti >= si, cmw_ref[2 * p + 1], 0.0).astype(BF16)
            stacked.append(jnp.concatenate([w0, w1], axis=0))
        chunks = []
        for c in range(R // CHUNK):
            tiles = []
            for p in range(D // LANES):
                res = _dot(stacked[p], vnb[c * CHUNK:(c + 1) * CHUNK, p * LANES:(p + 1) * LANES])
                tiles.append(jnp.where(low_lanes, res[:CHUNK], res[CHUNK:]))
            chunks.append(jnp.concatenate(tiles, axis=1) + cmb_ref[...])
        mix = jnp.concatenate(chunks, axis=0)
    else:
        vn_ref[...] = vn.reshape(nb, r, D)
        def diag(d):
            return jnp.broadcast_to(wd_ref[d][None], (nb, r, D)).reshape(R, D)
        mix = diag(0) * vn
        for d in range(1, r):
            mix = mix + diag(d) * pltpu.roll(vn, d, 0)
        mix = mix + jnp.broadcast_to(cmb_ref[...][None], (nb, r, D)).reshape(R, D)
    yb = (u * mix).astype(BF16)

    gb = zcol(3)
    q = zcol(4) * zcol(5)
    q_d = shifter(q, q_c if prompt else None, None if prompt else sc_ref)
    sw = sw_ref[...]
    yc = (gb * (q * sw[2:3] + q_d(1) * sw[1:2] + q_d(2) * sw[0:1])).astype(BF16)

    merged = (_sigmoid(zcol(6)) * _dot(ya, wbr_ref[0])
              + _sigmoid(zcol(7)) * _dot(yb, wbr_ref[1])
              + _sigmoid(zcol(8)) * _dot(yc, wbr_ref[2]))
    o = _dot(merged.astype(BF16), wo_ref[...])
    x1_ref[...] = x3 + gate1 * o.reshape(nb, r, D)

    hs_ref[...] = hs.reshape(nb, r, D)[:, r - SUBLANES:, :]
    xa_ref[...] = xa.reshape(nb, r, D)[:, r - SUBLANES:, :]
    q_ref[...] = q.reshape(nb, r, D)[:, r - SUBLANES:, :]


def _vmem_spec():
    return pl.BlockSpec(memory_space=pltpu.VMEM)


def _mixer_call(prompt, x, mod, state, w):
    nbatch, t, _ = x.shape
    if prompt:
        nb, r = 1, ROWS
        grid = (nbatch, t // r)
        tile = lambda b, j: (b, j, 0)
        per_b = lambda b, j: (b, 0, 0)
        sem = ("arbitrary", "arbitrary")
    else:
        nb, r = SAMPLE_MIXER_ROWS // t, t
        grid = (nbatch // nb,)
        tile = lambda i: (i, 0, 0)
        per_b = tile
        sem = ("arbitrary",)
    tail =jax.ShapeDtypeStruct((nbatch, SUBLANES, D), F32)
    in_specs = [pl.BlockSpec((nb, r, D), tile), pl.BlockSpec((nb, 1, 3 * D), per_b)]
    args = [x, mod]
    if not prompt:
        in_specs += [pl.BlockSpec((nb, 1, D), per_b), pl.BlockSpec((nb, SUBLANES, D), per_b),
                     pl.BlockSpec((nb, SUBLANES, D), per_b)]
        args += list(state)
    weights = [w["g1"], w["w_in"], w["conv_w"], w["conv_b"], w["wg"], w["ba"], w["bx"], w["lam"],
               w["ln_g"], w["ln_b"]]
    weights += [w["cm_ws"], w["cm_bs"]] if prompt else [w["cm_diag"], w["cm_bs"][:SUBLANES]]
    weights += [w["sc_w"], w["w_br"], w["w_o"]]
    in_specs += [_vmem_spec() for _ in weights]
    args += weights
    out_shape = [jax.ShapeDtypeStruct(x.shape, F32), tail, tail, tail]
    out_specs = [pl.BlockSpec((nb, r, D), tile)] + [pl.BlockSpec((nb, SUBLANES, D), per_b)] * 3
    scratch = []
    if prompt:
        scratch = [pltpu.VMEM((SUBLANES, D), F32)] * 3
    else:
        out_shape.append(jax.ShapeDtypeStruct(x.shape, F32))
        out_specs.append(pl.BlockSpec((nb, r, D), tile))
    return pl.pallas_call(
        functools.partial(_mixer_kernel, prompt, nb, r),
        grid=grid, in_specs=in_specs, out_specs=out_specs, out_shape=out_shape,
        scratch_shapes=scratch,
        compiler_params=pltpu.CompilerParams(dimension_semantics=sem, vmem_limit_bytes=VMEM_LIMIT),
        name="mixer_prompt" if prompt else "mixer_sample",
    )(*args)


def _ffn_kernel(final, nb, r, x_ref, mod_ref, g2_ref, w1_ref, w2_ref, gf_ref, o_ref):
    R = nb * r
    x3 = x_ref[...]
    m = mod_ref[...]
    shift2, scale2, gate2 = m[:, :, 0:D], m[:, :, D:2 * D], m[:, :, 2 * D:3 * D]
    h2 = ((_rms(x3) * g2_ref[...]) * (1.0 + scale2) + shift2).reshape(R, D).astype(BF16)
    acts = []
    for c in range(D_FF // D):
        t = jnp.maximum(_dot(h2, w1_ref[:, c * D:(c + 1) * D]), 0.0)
        acts.append((t * t).astype(BF16))
    ff = _dot(jnp.concatenate(acts, axis=1), w2_ref[...])
    x2 = x3 + gate2 * ff.reshape(nb, r, D)
    if final:
        x2 = _rms(x2) * gf_ref[...]
    o_ref[...] = x2


def _ffn_call(prompt, final, x, mod, w, g_final):
    nbatch, t, _ = x.shape
    if prompt:
        nb, r = 1, ROWS
        grid = (nbatch, t // r)
        tile = lambda b, j: (b, j, 0)
        per_b = lambda b, j: (b, 0, 1)
        sem = ("arbitrary", "arbitrary")
    else:
        nb, r = ROWS // t, t
        grid = (nbatch // nb,)
        tile = lambda i: (i, 0, 0)
        per_b = lambda i: (i, 0, 1)
        sem = ("arbitrary",)
    return pl.pallas_call(
        functools.partial(_ffn_kernel, final, nb, r),
        grid=grid,
        in_specs=[pl.BlockSpec((nb, r, D), tile), pl.BlockSpec((nb, 1, 3 * D), per_b),
                  _vmem_spec(), _vmem_spec(), _vmem_spec(), _vmem_spec()],
        out_specs=pl.BlockSpec((nb, r, D), tile),
        out_shape=jax.ShapeDtypeStruct(x.shape, F32),
        compiler_params=pltpu.CompilerParams(dimension_semantics=sem, vmem_limit_bytes=VMEM_LIMIT),
        name="ffn_prompt" if prompt else "ffn_sample",
    )(x, mod, w["g2"], w["w_ff1"], w["w_ff2"], g_final)


def _gate_weights(wa, wx):
    depth, heads, hd, _ = wa.shape
    per_tile = GATE_TILE // hd
    eye = jnp.eye(per_tile, dtype=wa.dtype)

    def blockdiag(wh):
        w5 = wh.reshape(depth, heads // per_tile, per_tile, hd, hd)
        out = jnp.einsum("ljhik,hg->ljhigk", w5, eye)
        return out.reshape(depth, heads // per_tile, GATE_TILE, GATE_TILE)

    return jnp.concatenate([blockdiag(wa), blockdiag(wx)], axis=-1).astype(BF16)


def _chunk_diagonals(cm_ws, steps):
    t = jnp.arange(steps)[None, :, None]
    d = jnp.arange(steps)[:, None, None]
    s = jnp.arange(steps)[None, None, :]
    on_diag = s == t - d
    corner = cm_ws[:, :, :steps, :steps]
    vals = jnp.sum(jnp.where(on_diag[None, None], corner[:, :, None], 0.0), axis=-1)
    vals = jnp.transpose(vals, (0, 2, 3, 1))
    return jnp.repeat(vals, GROUP_W, axis=-1)


def kernel(x_prompt, x_sample, c_prompt, c_sample, state_lru_h, state_lru_conv, state_sconv, w_ada, b_ada, g_norm1, g_norm2, g_final, w_in, lru_conv_w, lru_conv_b, lru_wa, lru_ba, lru_wx, lru_bx, lru_lambda, cm_ln_g, cm_ln_b, cm_ws, cm_bs, sc_conv_w, w_br_a, w_br_b, w_br_c, w_o, w_ff1, w_ff2):
    depth = w_in.shape[0]
    bp = x_prompt.shape[0]
    bs, ts, _ = x_sample.shape
    assert x_prompt.shape[1] % ROWS == 0 and ROWS % ts == 0 and bs % (ROWS // ts) == 0
    assert ts == SUBLANES and bs % (SAMPLE_MIXER_ROWS // ts) == 0

    mod = _modulation(jnp.concatenate([c_prompt, c_sample], axis=0), w_ada, b_ada)

    w_in_b = w_in.astype(BF16)
    w_br_b16 = jnp.stack([w_br_a, w_br_b, w_br_c], axis=1).astype(BF16)
    w_o_b = w_o.astype(BF16)
    w_ff1_b = w_ff1.astype(BF16)
    w_ff2_b = w_ff2.astype(BF16)
    wg = _gate_weights(lru_wa, lru_wx)
    cm_bs_e = jnp.repeat(jnp.transpose(cm_bs, (0, 2, 1)), GROUP_W, axis=-1)
    cm_diag = _chunk_diagonals(cm_ws, ts)
    row = lambda p: p.reshape(1, D)
    g_final2 = row(g_final)
    pad_state = lambda s: jnp.pad(s, ((0, 0), (SUBLANES - s.shape[1], 0), (0, 0)))

    xp, xs = x_prompt, x_sample
    prompt_tails, sample_tails, vs_l = [], [], []
    for l in range(depth):
        w = dict(g1=row(g_norm1[l]), g2=row(g_norm2[l]), w_in=w_in_b[l], conv_w=lru_conv_w[l],
                 conv_b=row(lru_conv_b[l]), wg=wg[l], ba=row(lru_ba[l]), bx=row(lru_bx[l]),
                 lam=row(lru_lambda[l]), ln_g=row(cm_ln_g[l]), ln_b=row(cm_ln_b[l]),
                 cm_ws=cm_ws[l], cm_bs=cm_bs_e[l], cm_diag=cm_diag[l], sc_w=sc_conv_w[l],
                 w_br=w_br_b16[l], w_o=w_o_b[l], w_ff1=w_ff1_b[l], w_ff2=w_ff2_b[l])
        final = l == depth - 1
        mod_p = mod[l, :bp].reshape(bp, 1, N_MOD * D)
        mod_s = mod[l, bp:].reshape(bs, 1, N_MOD * D)

        xp, hp, lcp, scp = _mixer_call(True, xp, mod_p, None, w)
        xp = _ffn_call(True, final, xp, mod_p, w, g_final2)
        prompt_tails.append((hp, lcp, scp))

        state = (state_lru_h[l].reshape(bs, 1, D), pad_state(state_lru_conv[l]),
                 pad_state(state_sconv[l]))
        xs, hs, lcs, scs, vs = _mixer_call(False, xs, mod_s, state, w)
        xs = _ffn_call(False, final, xs, mod_s, w, g_final2)
        sample_tails.append((hs, lcs, scs))
        vs_l.append(vs)

    def assemble(tails):
        h = jnp.stack([t[0][:, SUBLANES - 1] for t in tails])
        lc = jnp.stack([t[1][:, SUBLANES - 3:] for t in tails])
        sc = jnp.stack([t[2][:, SUBLANES - 2:] for t in tails])
        return h, lc, sc

    return (xp, xs) + assemble(prompt_tails) + assemble(sample_tails) + (jnp.stack(vs_l),)
```

```python
import functools

import jax
import jax.numpy as jnp
from jax import lax
from jax.experimental import pallas as pl
from jax.experimental.pallas import tpu as pltpu

D = 1024
D_FF = 4 * D
N_MOD = 6
EPS = 1e-6
LRU_C = 8.0
CHUNK = 128
N_GROUPS = 16
GROUP_W = D // N_GROUPS
SUBLANES = 8
LANES = 128
GATE_TILE = 256
ROWS = 512
SAMPLE_MIXER_ROWS = 256
VMEM_LIMIT = 60 * 1024 * 1024

(P_G1, P_G2, P_CONV_B, P_BA, P_BX, P_LAM, P_LN_G, P_LN_B, P_CONV_W, P_SC_W, P_ROWS) = (
    0, 1, 2, 3, 4, 5, 6, 7, 8, 12, 16)

F32 = jnp.float32
BF16 = jnp.bfloat16


def _sigmoid(x):
    return 1.0 / (1.0 + jnp.exp(-x))


def _dot(a, b):
    return jnp.dot(a, b, preferred_element_type=F32)


def _rms(x):
    ms = jnp.mean(x * x, axis=-1, keepdims=True)
    return x * lax.rsqrt(ms + EPS)


def _expand_rows(m, reps):
    n, w = m.shape
    return jnp.concatenate([jnp.broadcast_to(m[b:b + 1], (reps, w)) for b in range(n)], axis=0)


def _roll_in_groups(x, shift):
    rows, w = x.shape
    x3 = x.reshape(rows // SUBLANES, SUBLANES, w)
    return pltpu.roll(x3, shift, 1).reshape(rows, w)


def _mod_kernel(c_ref, w_ref, b_ref, o_ref):
    c = c_ref[...]
    s = (c * _sigmoid(c)).astype(BF16)
    o_ref[...] = _dot(s, w_ref[...].astype(BF16)) + b_ref[...]


def _modulation(c_all, w_ada, b_ada):
    depth = w_ada.shape[0]
    n = c_all.shape[0]
    return pl.pallas_call(
        _mod_kernel,
        grid=(depth, N_MOD),
        in_specs=[
            pl.BlockSpec((n, D), lambda l, j: (0, 0)),
            pl.BlockSpec((None, D, D), lambda l, j: (l, 0, j)),
            pl.BlockSpec((None, 1, D), lambda l, j: (l, 0, j)),
        ],
        out_specs=pl.BlockSpec((None, n, D), lambda l, j: (l, 0, j)),
        out_shape=jax.ShapeDtypeStruct((depth, n, N_MOD * D), F32),
        compiler_params=pltpu.CompilerParams(
            dimension_semantics=("arbitrary", "arbitrary"), vmem_limit_bytes=VMEM_LIMIT),
        name="modulation",
    )(c_all, w_ada, b_ada.reshape(depth, 1, N_MOD * D))


def _group_scan(a, b, row8):
    for s in (1, 2, 4):
        keep = row8 >= s
        a_sh = jnp.where(keep, _roll_in_groups(a, s), 1.0)
        b_sh = jnp.where(keep, _roll_in_groups(b, s), 0.0)
        b = a * b_sh + b
        a = a * a_sh
    return a, b


def _mixer_kernel(prompt, R, *refs):
    if prompt:
        (x_ref, mod_ref, p_ref, win_ref, wg_ref, cmw_ref, cmb_ref, wbra_ref, wbrb_ref, wbrc_ref,
         wo_ref, x1_ref, hs_ref, xa_ref, q_ref, xa_c, q_c, h_c) = refs
    else:
        (x_ref, mod_ref, h0_ref, lc_ref, sc_ref, p_ref, win_ref, wg_ref, wd_ref, cmb_ref,
         wbra_ref, wbrb_ref, wbrc_ref, wo_ref, x1_ref, hs_ref, xa_ref, q_ref, vn_ref) = refs

    if prompt:
        @pl.when(pl.program_id(1) == 0)
        def _():
            xa_c[...] = jnp.zeros_like(xa_c)
            q_c[...] = jnp.zeros_like(q_c)
            h_c[...] = jnp.zeros_like(h_c)
        m = mod_ref[pl.ds(pl.program_id(0), 1), :]
    else:
        m = _expand_rows(mod_ref[...], SUBLANES)
    shift1, scale1, gate1 = m[:, 0:D], m[:, D:2 * D], m[:, 2 * D:3 * D]

    def prow(i, n=1):
        return p_ref[i:i + n, :]

    x = x_ref[...]
    h = ((_rms(x) * prow(P_G1)) * (1.0 + scale1) + shift1).astype(BF16)

    def zcol(k):
        return _dot(h, win_ref[:, k * D:(k + 1) * D])

    row8 = lax.broadcasted_iota(jnp.int32, (R, D), 0) % SUBLANES

    def shifter(val, carry_ref, state_ref):
        if prompt:
            ext = jnp.concatenate([carry_ref[...], val], axis=0)
            carry_ref[...] = val[R - SUBLANES:]
            return lambda j: pltpu.roll(ext, j, 0)[SUBLANES:]
        past = state_ref[...]
        return lambda j: jnp.where(row8 >= j, _roll_in_groups(val, j), _roll_in_groups(past, j))

    xa = zcol(0)
    u = zcol(1)
    v = zcol(2)

    xa_d = shifter(xa, xa_c if prompt else None, None if prompt else lc_ref)
    cw = prow(P_CONV_W, 4)
    xc = (xa * cw[3:4] + xa_d(1) * cw[2:3] + xa_d(2) * cw[1:2] + xa_d(3) * cw[0:1]
          + prow(P_CONV_B))
    xcb = xc.astype(BF16)
    pre_r, pre_i = [], []
    for j in range(D // GATE_TILE):
        g = _dot(xcb[:, j * GATE_TILE:(j + 1) * GATE_TILE], wg_ref[j])
        pre_r.append(g[:, :GATE_TILE])
        pre_i.append(g[:, GATE_TILE:])

    gb = zcol(3)
    gc = zcol(4)
    xcc = zcol(5)

    gate_r = _sigmoid(jnp.concatenate(pre_r, axis=1) + prow(P_BA))
    gate_i = _sigmoid(jnp.concatenate(pre_i, axis=1) + prow(P_BX))
    nl = -prow(P_LAM)
    softplus = jnp.maximum(nl, 0.0) + jnp.log1p(jnp.exp(-jnp.abs(nl)))
    a = jnp.exp((-LRU_C * softplus) * gate_r)
    om = 1.0 - a * a
    bt = jnp.where(om > 0.0, om * lax.rsqrt(om), 0.0) * (gate_i * xc)
    A, B = _group_scan(a, bt, row8)
    if prompt:
        carry = h_c[...]
        pieces = []
        for i in range(R // SUBLANES):
            sl = slice(i * SUBLANES, (i + 1) * SUBLANES)
            hi = A[sl] * carry + B[sl]
            pieces.append(hi)
            carry = jnp.broadcast_to(hi[SUBLANES - 1:SUBLANES], (SUBLANES, D))
        hs = jnp.concatenate(pieces, axis=0)
        h_c[...] = carry
    else:
        hs = A * _expand_rows(h0_ref[...], SUBLANES) + B
    ya = hs.astype(BF16)

    mu = jnp.mean(v, axis=-1, keepdims=True)
    vc = v - mu
    var = jnp.mean(vc * vc, axis=-1, keepdims=True)
    vn = vc * lax.rsqrt(var + EPS) * prow(P_LN_G) + prow(P_LN_B)
    if prompt:
        vnb = vn.astype(BF16)
        ti = lax.broadcasted_iota(jnp.int32, (CHUNK, CHUNK), 0)
        si = lax.broadcasted_iota(jnp.int32, (CHUNK, CHUNK), 1)
        low_lanes = lax.broadcasted_iota(jnp.int32, (CHUNK, LANES), 1) < GROUP_W
        stacked = []
        for p in range(D // LANES):
            w0 = jnp.where(ti >= si, cmw_ref[2 * p], 0.0).astype(BF16)
            w1 = jnp.where(ti >= si, cmw_ref[2 * p + 1], 0.0).astype(BF16)
            stacked.append(jnp.concatenate([w0, w1], axis=0))
        chunks = []
        for c in range(R // CHUNK):
            tiles = []
            for p in range(D // LANES):
                res = _dot(stacked[p], vnb[c * CHUNK:(c + 1) * CHUNK, p * LANES:(p + 1) * LANES])
                tiles.append(jnp.where(low_lanes, res[:CHUNK], res[CHUNK:]))
            chunks.append(jnp.concatenate(tiles, axis=1) + cmb_ref[...])
        mix = jnp.concatenate(chunks, axis=0)
    else:
        vn_ref[...] = vn
        def tiled(w):
            return jnp.broadcast_to(w[None], (R // SUBLANES, SUBLANES, D)).reshape(R, D)
        mix = tiled(wd_ref[0]) * vn
        for d in range(1, SUBLANES):
            mix = mix + tiled(wd_ref[d]) * _roll_in_groups(vn, d)
        mix = mix + tiled(cmb_ref[0:SUBLANES, :])

    g_a = zcol(6)
    g_b = zcol(7)
    g_c = zcol(8)

    yb = (u * mix).astype(BF16)

    q = gc * xcc
    q_d = shifter(q, q_c if prompt else None, None if prompt else sc_ref)
    sw = prow(P_SC_W, 3)
    yc = (gb * (q * sw[2:3] + q_d(1) * sw[1:2] + q_d(2) * sw[0:1])).astype(BF16)

    merged = (_sigmoid(g_a) * _dot(ya, wbra_ref[...])
              + _sigmoid(g_b) * _dot(yb, wbrb_ref[...])
              + _sigmoid(g_c) * _dot(yc, wbrc_ref[...]))
    x1_ref[...] = x + gate1 * _dot(merged.astype(BF16), wo_ref[...])

    hs_ref[...] = hs[R - hs_ref.shape[0]:]
    xa_ref[...] = xa[R - xa_ref.shape[0]:]
    q_ref[...] = q[R - q_ref.shape[0]:]


def _resident(arr, l):
    nd = arr.ndim - 1
    return pl.BlockSpec((None,) + arr.shape[1:], lambda *_: (l,) + (0,) * nd,
                        pipeline_mode=pl.Buffered(1))


def _mixer_call(prompt, l, x, mod, state, w):
    if prompt:
        nbatch, t, _ = x.shape
        R = ROWS
        grid = (nbatch, t // R)
        x_spec = pl.BlockSpec((None, R, D), lambda b, j: (b, j, 0))
        mod_rows = mod.shape[1] - nbatch
        mod_spec = pl.BlockSpec((None, nbatch, 3 * D), lambda b, j: (l, mod_rows // nbatch, 0))
        tail_shape = jax.ShapeDtypeStruct((nbatch, SUBLANES, D), F32)
        tail_spec = pl.BlockSpec((None, SUBLANES, D), lambda b, j: (b, 0, 0))
        sem = ("arbitrary", "arbitrary")
    else:
        rows = x.shape[0]
        R = SAMPLE_MIXER_ROWS
        nb = R // SUBLANES
        grid = (rows // R,)
        x_spec = pl.BlockSpec((R, D), lambda i: (i, 0))
        mod_spec = pl.BlockSpec((None, nb, 3 * D), lambda i: (l, i, 0))
        tail_shape = jax.ShapeDtypeStruct((rows, D), F32)
        tail_spec = x_spec
        sem = ("arbitrary",)
    in_specs = [x_spec, mod_spec]
    args = [x, mod]
    if not prompt:
        in_specs += [pl.BlockSpec((nb, D), lambda i: (i, 0)), x_spec, x_spec]
        args += list(state)
    weights = [w["vec"], w["w_in"], w["wg"]]
    weights += [w["cm_ws"], w["cm_bs"]] if prompt else [w["cm_diag"], w["cm_bs"]]
    weights += [w["w_br_a"], w["w_br_b"], w["w_br_c"], w["w_o"]]
    in_specs += [_resident(a, l) for a in weights]
    args += weights
    out_shape = [jax.ShapeDtypeStruct(x.shape, F32), tail_shape, tail_shape, tail_shape]
    out_specs = [x_spec, tail_spec, tail_spec, tail_spec]
    scratch = []
    if prompt:
        scratch = [pltpu.VMEM((SUBLANES, D), F32)] * 3
    else:
        out_shape.append(jax.ShapeDtypeStruct(x.shape, F32))
        out_specs.append(x_spec)
    return pl.pallas_call(
        functools.partial(_mixer_kernel, prompt, R),
        grid=grid, in_specs=in_specs, out_specs=out_specs, out_shape=out_shape,
        scratch_shapes=scratch,
        compiler_params=pltpu.CompilerParams(dimension_semantics=sem, vmem_limit_bytes=VMEM_LIMIT),
        name="mixer_prompt" if prompt else "mixer_sample",
    )(*args)


def _ffn_kernel(prompt, final, R, x_ref, mod_ref, p_ref, w1_ref, w2_ref, gf_ref, o_ref):
    if prompt:
        m = mod_ref[pl.ds(pl.program_id(0), 1), :]
    else:
        m = _expand_rows(mod_ref[...], SUBLANES)
    shift2, scale2, gate2 = m[:, 0:D], m[:, D:2 * D], m[:, 2 * D:3 * D]
    x = x_ref[...]
    h2 = ((_rms(x) * p_ref[P_G2:P_G2 + 1, :]) * (1.0 + scale2) + shift2).astype(BF16)
    acts = []
    for c in range(D_FF // D):
        t = jnp.maximum(_dot(h2, w1_ref[:, c * D:(c + 1) * D]), 0.0)
        acts.append((t * t).astype(BF16))
    x2 = x + gate2 * _dot(jnp.concatenate(acts, axis=1), w2_ref[...])
    if final:
        x2 = _rms(x2) * gf_ref[...]
    o_ref[...] = x2


def _ffn_call(prompt, final, l, x, mod, w, g_final):
    if prompt:
        nbatch, t, _ = x.shape
        R = ROWS
        grid = (nbatch, t // R)
        x_spec = pl.BlockSpec((None, R, D), lambda b, j: (b, j, 0))
        mod_rows = mod.shape[1] - nbatch
        mod_spec = pl.BlockSpec((None, nbatch, 3 * D), lambda b, j: (l, mod_rows // nbatch, 1))
        sem = ("arbitrary", "arbitrary")
    else:
        R = ROWS
        grid = (x.shape[0] // R,)
        x_spec = pl.BlockSpec((R, D), lambda i: (i, 0))
        mod_spec = pl.BlockSpec((None, R // SUBLANES, 3 * D), lambda i: (l, i, 1))
        sem = ("arbitrary",)
    return pl.pallas_call(
        functools.partial(_ffn_kernel, prompt, final, R),
        grid=grid,
        in_specs=[x_spec, mod_spec, _resident(w["vec"], l), _resident(w["w_ff1"], l),
                  _resident(w["w_ff2"], l), pl.BlockSpec((1, D), lambda *_: (0, 0))],
        out_specs=x_spec,
        out_shape=jax.ShapeDtypeStruct(x.shape, F32),
        compiler_params=pltpu.CompilerParams(dimension_semantics=sem, vmem_limit_bytes=VMEM_LIMIT),
        name="ffn_prompt" if prompt else "ffn_sample",
    )(x, mod, w["vec"], w["w_ff1"], w["w_ff2"], g_final)


def _gate_weights(wa, wx):
    depth, heads, hd, _ = wa.shape
    per_tile = GATE_TILE // hd
    eye = jnp.eye(per_tile, dtype=wa.dtype)

    def blockdiag(wh):
        w5 = wh.reshape(depth, heads // per_tile, per_tile, hd, hd)
        out = w5[:, :, :, :, None, :] * eye[None, None, :, None, :, None]
        return out.reshape(depth, heads // per_tile, GATE_TILE, GATE_TILE)

    return jnp.concatenate([blockdiag(wa), blockdiag(wx)], axis=-1).astype(BF16)


def _chunk_diagonals(cm_ws, steps):
    t = jnp.arange(steps)[None, :, None]
    d = jnp.arange(steps)[:, None, None]
    s = jnp.arange(steps)[None, None, :]
    on_diag = s == t - d
    corner = cm_ws[:, :, :steps, :steps]
    vals = jnp.sum(jnp.where(on_diag[None, None], corner[:, :, None], 0.0), axis=-1)
    vals = jnp.transpose(vals, (0, 2, 3, 1))
    return jnp.repeat(vals, GROUP_W, axis=-1)


def kernel(x_prompt, x_sample, c_prompt, c_sample, state_lru_h, state_lru_conv, state_sconv, w_ada, b_ada, g_norm1, g_norm2, g_final, w_in, lru_conv_w, lru_conv_b, lru_wa, lru_ba, lru_wx, lru_bx, lru_lambda, cm_ln_g, cm_ln_b, cm_ws, cm_bs, sc_conv_w, w_br_a, w_br_b, w_br_c, w_o, w_ff1, w_ff2):
    depth = w_in.shape[0]
    bp = x_prompt.shape[0]
    bs, ts, _ = x_sample.shape
    assert x_prompt.shape[1] % ROWS == 0 and (bs * ts) % ROWS == 0
    assert ts == SUBLANES and (bs * ts) % SAMPLE_MIXER_ROWS == 0 and bs % bp == 0

    mod = _modulation(jnp.concatenate([c_sample, c_prompt], axis=0), w_ada, b_ada)

    vec = jnp.concatenate(
        [p[:, None, :] for p in (g_norm1, g_norm2, lru_conv_b, lru_ba, lru_bx, lru_lambda,
                                  cm_ln_g, cm_ln_b)]
        + [lru_conv_w, sc_conv_w, jnp.zeros((depth, P_ROWS - P_SC_W - sc_conv_w.shape[1], D), F32)],
        axis=1)
    w = dict(vec=vec, w_in=w_in.astype(BF16), wg=_gate_weights(lru_wa, lru_wx), cm_ws=cm_ws,
             cm_bs=jnp.repeat(jnp.transpose(cm_bs, (0, 2, 1)), GROUP_W, axis=-1),
             cm_diag=_chunk_diagonals(cm_ws, ts),
             w_br_a=w_br_a.astype(BF16), w_br_b=w_br_b.astype(BF16), w_br_c=w_br_c.astype(BF16),
             w_o=w_o.astype(BF16), w_ff1=w_ff1.astype(BF16), w_ff2=w_ff2.astype(BF16))
    g_final2 = g_final.reshape(1, D)

    def pad_state(s):
        return jnp.pad(s, ((0, 0), (SUBLANES - s.shape[1], 0), (0, 0))).reshape(bs * SUBLANES, D)

    xp, xs = x_prompt, x_sample.reshape(bs * ts, D)
    prompt_tails, sample_tails, vs_l = [], [], []
    for l in range(depth):
        final = l == depth - 1
        xp, hp, lcp, scp = _mixer_call(True, l, xp, mod, None, w)
        xp = _ffn_call(True, final, l, xp, mod, w, g_final2)
        prompt_tails.append((hp, lcp, scp))

        state = (state_lru_h[l], pad_state(state_lru_conv[l]), pad_state(state_sconv[l]))
        xs, hs, lcs, scs, vs = _mixer_call(False, l, xs, mod, state, w)
        xs = _ffn_call(False, final, l, xs, mod, w, g_final2)
        sample_tails.append(tuple(t.reshape(bs, ts, D) for t in (hs, lcs, scs)))
        vs_l.append(vs.reshape(bs, ts, D))

    def assemble(tails):
        h = jnp.stack([t[0][:, SUBLANES - 1] for t in tails])
        lc = jnp.stack([t[1][:, SUBLANES - 3:] for t in tails])
        sc = jnp.stack([t[2][:, SUBLANES - 2:] for t in tails])
        return h, lc, sc

    return ((xp, xs.reshape(bs, ts, D)) + assemble(prompt_tails) + assemble(sample_tails)
            + (jnp.stack(vs_l),))
```

```python
import functools

import jax
import jax.numpy as jnp
from jax import lax
from jax.experimental import pallas as pl
from jax.experimental.pallas import tpu as pltpu

D = 1024
D_FF = 4 * D
N_MOD = 6
EPS = 1e-6
LRU_C = 8.0
CHUNK = 128
N_GROUPS = 16
GROUP_W = D // N_GROUPS
SUBLANES = 8
LANES = 128
GATE_TILE = 256
ROWS = 512
SAMPLE_MIXER_ROWS = 256
VMEM_LIMIT = 60 * 1024 * 1024

(P_G1, P_G2, P_CONV_B, P_BA, P_BX, P_LAM, P_LN_G, P_LN_B, P_CONV_W, P_SC_W, P_ROWS) = (
    0, 1, 2, 3, 4, 5, 6, 7, 8, 12, 16)

F32 = jnp.float32
BF16 = jnp.bfloat16


def _sigmoid(x):
    return 1.0 / (1.0 + jnp.exp(-x))


def _dot(a, b):
    return jnp.dot(a, b, preferred_element_type=F32)


def _wdot(a, w_packed):
    return _dot(a, pltpu.bitcast(w_packed, BF16))


def _rms(x):
    ms = jnp.mean(x * x, axis=-1, keepdims=True)
    return x * lax.rsqrt(ms + EPS)


def _expand_rows(m, reps):
    n, w = m.shape
    return jnp.concatenate([jnp.broadcast_to(m[b:b + 1], (reps, w)) for b in range(n)], axis=0)


def _roll_in_groups(x, shift):
    rows, w = x.shape
    x3 = x.reshape(rows // SUBLANES, SUBLANES, w)
    return pltpu.roll(x3, shift, 1).reshape(rows, w)


def _mod_kernel(c_ref, w_ref, b_ref, o_ref):
    c = c_ref[...]
    s = (c * _sigmoid(c)).astype(BF16)
    o_ref[...] = _dot(s, w_ref[...].astype(BF16)) + b_ref[...]


def _modulation(c_all, w_ada, b_ada):
    depth = w_ada.shape[0]
    n = c_all.shape[0]
    return pl.pallas_call(
        _mod_kernel,
        grid=(depth, N_MOD),
        in_specs=[
            pl.BlockSpec((n, D), lambda l, j: (0, 0)),
            pl.BlockSpec((None, D, D), lambda l, j: (l, 0, j)),
            pl.BlockSpec((None, 1, D), lambda l, j: (l, 0, j)),
        ],
        out_specs=pl.BlockSpec((None, n, D), lambda l, j: (l, 0, j)),
        out_shape=jax.ShapeDtypeStruct((depth, n, N_MOD * D), F32),
        compiler_params=pltpu.CompilerParams(
            dimension_semantics=("arbitrary", "arbitrary"), vmem_limit_bytes=VMEM_LIMIT),
        name="modulation",
    )(c_all, w_ada, b_ada.reshape(depth, 1, N_MOD * D))


def _group_scan(a, b, row8):
    for s in (1, 2, 4):
        keep = row8 >= s
        a_sh = jnp.where(keep, _roll_in_groups(a, s), 1.0)
        b_sh = jnp.where(keep, _roll_in_groups(b, s), 0.0)
        b = a * b_sh + b
        a = a * a_sh
    return a, b


def _mixer_kernel(prompt, R, *refs):
    if prompt:
        (x_ref, mod_ref, p_ref, win_ref, wg_ref, cmw_ref, cmb_ref, wbra_ref, wbrb_ref, wbrc_ref,
         wo_ref, x1_ref, hs_ref, xa_ref, q_ref, xa_c, q_c, h_c) = refs
    else:
        (x_ref, mod_ref, h0_ref, lc_ref, sc_ref, p_ref, win_ref, wg_ref, wd_ref, cmb_ref,
         wbra_ref, wbrb_ref, wbrc_ref, wo_ref, x1_ref, hs_ref, xa_ref, q_ref, vn_ref) = refs

    if prompt:
        @pl.when(pl.program_id(1) == 0)
        def _():
            xa_c[...] = jnp.zeros_like(xa_c)
            q_c[...] = jnp.zeros_like(q_c)
            h_c[...] = jnp.zeros_like(h_c)
        m = mod_ref[pl.ds(pl.program_id(0), 1), :]
    else:
        m = _expand_rows(mod_ref[...], SUBLANES)
    shift1, scale1, gate1 = m[:, 0:D], m[:, D:2 * D], m[:, 2 * D:3 * D]

    def prow(i, n=1):
        return p_ref[i:i + n, :]

    x = x_ref[...]
    h = ((_rms(x) * prow(P_G1)) * (1.0 + scale1) + shift1).astype(BF16)

    def zcol(k):
        return _wdot(h, win_ref[:, k * D:(k + 1) * D])

    row8 = lax.broadcasted_iota(jnp.int32, (R, D), 0) % SUBLANES

    def shifter(val, carry_ref, state_ref):
        if prompt:
            ext = jnp.concatenate([carry_ref[...], val], axis=0)
            carry_ref[...] = val[R - SUBLANES:]
            return lambda j: pltpu.roll(ext, j, 0)[SUBLANES:]
        past = state_ref[...]
        return lambda j: jnp.where(row8 >= j, _roll_in_groups(val, j), _roll_in_groups(past, j))

    xa = zcol(0)
    u = zcol(1)

    xa_d = shifter(xa, xa_c if prompt else None, None if prompt else lc_ref)
    cw = prow(P_CONV_W, 4)
    xc = (xa * cw[3:4] + xa_d(1) * cw[2:3] + xa_d(2) * cw[1:2] + xa_d(3) * cw[0:1]
          + prow(P_CONV_B))
    xcb = xc.astype(BF16)
    pre_r, pre_i = [], []
    for j in range(D // GATE_TILE):
        g = _wdot(xcb[:, j * GATE_TILE:(j + 1) * GATE_TILE], wg_ref[j])
        pre_r.append(g[:, :GATE_TILE])
        pre_i.append(g[:, GATE_TILE:])

    v = zcol(2)

    gate_r = _sigmoid(jnp.concatenate(pre_r, axis=1) + prow(P_BA))
    gate_i = _sigmoid(jnp.concatenate(pre_i, axis=1) + prow(P_BX))
    nl = -prow(P_LAM)
    softplus = jnp.maximum(nl, 0.0) + jnp.log1p(jnp.exp(-jnp.abs(nl)))
    a = jnp.exp((-LRU_C * softplus) * gate_r)
    om = 1.0 - a * a
    bt = jnp.where(om > 0.0, om * lax.rsqrt(om), 0.0) * (gate_i * xc)

    gb = zcol(3)

    A, B = _group_scan(a, bt, row8)
    if prompt:
        carry = h_c[...]
        pieces = []
        for i in range(R // SUBLANES):
            sl = slice(i * SUBLANES, (i + 1) * SUBLANES)
            hi = A[sl] * carry + B[sl]
            pieces.append(hi)
            carry = jnp.broadcast_to(hi[SUBLANES - 1:SUBLANES], (SUBLANES, D))
        hs = jnp.concatenate(pieces, axis=0)
        h_c[...] = carry
    else:
        hs = A * _expand_rows(h0_ref[...], SUBLANES) + B
    ya = hs.astype(BF16)

    gc = zcol(4)

    mu = jnp.mean(v, axis=-1, keepdims=True)
    vc = v - mu
    var = jnp.mean(vc * vc, axis=-1, keepdims=True)
    vn = vc * lax.rsqrt(var + EPS) * prow(P_LN_G) + prow(P_LN_B)
    if prompt:
        low_lanes = lax.broadcasted_iota(jnp.int32, (R, D), 1) % LANES < GROUP_W
        vn_lo = jnp.where(low_lanes, vn, 0.0).astype(BF16)
        vn_hi = jnp.where(low_lanes, 0.0, vn).astype(BF16)
    else:
        vn_ref[...] = vn

    xcc = zcol(5)

    q = gc * xcc
    q_d = shifter(q, q_c if prompt else None, None if prompt else sc_ref)
    sw = prow(P_SC_W, 3)
    yc = (gb * (q * sw[2:3] + q_d(1) * sw[1:2] + q_d(2) * sw[0:1])).astype(BF16)

    g_a = zcol(6)

    if prompt:
        ti = lax.broadcasted_iota(jnp.int32, (CHUNK, CHUNK), 0)
        si = lax.broadcasted_iota(jnp.int32, (CHUNK, CHUNK), 1)
        paired = []
        for p in range(D // LANES):
            w0 = jnp.where(ti >= si, cmw_ref[2 * p], 0.0).astype(BF16)
            w1 = jnp.where(ti >= si, cmw_ref[2 * p + 1], 0.0).astype(BF16)
            paired.append(jnp.concatenate([w0, w1], axis=1))
        chunks = []
        for c in range(R // CHUNK):
            rows = slice(c * CHUNK, (c + 1) * CHUNK)
            tiles = []
            for p in range(D // LANES):
                lanes = slice(p * LANES, (p + 1) * LANES)
                rhs = jnp.concatenate([vn_lo[rows, lanes], vn_hi[rows, lanes]], axis=0)
                tiles.append(_dot(paired[p], rhs))
            chunks.append(jnp.concatenate(tiles, axis=1) + cmb_ref[...])
        mix = jnp.concatenate(chunks, axis=0)
    else:
        def tiled(w):
            return jnp.broadcast_to(w[None], (R // SUBLANES, SUBLANES, D)).reshape(R, D)
        mix = tiled(wd_ref[0]) * vn
        for d in range(1, SUBLANES):
            mix = mix + tiled(wd_ref[d]) * _roll_in_groups(vn, d)
        mix = mix + tiled(cmb_ref[0:SUBLANES, :])

    g_b = zcol(7)

    yb = (u * mix).astype(BF16)

    g_c = zcol(8)

    merged = (_sigmoid(g_a) * _wdot(ya, wbra_ref[...])
              + _sigmoid(g_b) * _wdot(yb, wbrb_ref[...])
              + _sigmoid(g_c) * _wdot(yc, wbrc_ref[...]))
    x1_ref[...] = x + gate1 * _wdot(merged.astype(BF16), wo_ref[...])

    hs_ref[...] = hs[R - hs_ref.shape[0]:]
    xa_ref[...] = xa[R - xa_ref.shape[0]:]
    q_ref[...] = q[R - q_ref.shape[0]:]


def _resident(arr, l):
    nd = arr.ndim - 1
    return pl.BlockSpec((None,) + arr.shape[1:], lambda *_: (l,) + (0,) * nd,
                        pipeline_mode=pl.Buffered(1))


def _mixer_call(prompt, l, x, mod, state, w):
    if prompt:
        nbatch, t, _ = x.shape
        R = ROWS
        grid = (nbatch, t // R)
        x_spec = pl.BlockSpec((None, R, D), lambda b, j: (b, j, 0))
        mod_rows = mod.shape[1] - nbatch
        mod_spec = pl.BlockSpec((None, nbatch, 3 * D), lambda b, j: (l, mod_rows // nbatch, 0))
        tail_shape = jax.ShapeDtypeStruct((nbatch, SUBLANES, D), F32)
        tail_spec = pl.BlockSpec((None, SUBLANES, D), lambda b, j: (b, 0, 0))
        sem = ("arbitrary", "arbitrary")
    else:
        rows = x.shape[0]
        R = SAMPLE_MIXER_ROWS
        nb = R // SUBLANES
        grid = (rows // R,)
        x_spec = pl.BlockSpec((R, D), lambda i: (i, 0))
        mod_spec = pl.BlockSpec((None, nb, 3 * D), lambda i: (l, i, 0))
        tail_shape = jax.ShapeDtypeStruct((rows, D), F32)
        tail_spec = x_spec
        sem = ("arbitrary",)
    in_specs = [x_spec, mod_spec]
    args = [x, mod]
    if not prompt:
        in_specs += [pl.BlockSpec((nb, D), lambda i: (i, 0)), x_spec, x_spec]
        args += list(state)
    weights = [w["vec"], w["w_in"], w["wg"]]
    weights += [w["cm_ws"], w["cm_bs"]] if prompt else [w["cm_diag"], w["cm_bs"]]
    weights += [w["w_br_a"], w["w_br_b"], w["w_br_c"], w["w_o"]]
    in_specs += [_resident(a, l) for a in weights]
    args += weights
    out_shape = [jax.ShapeDtypeStruct(x.shape, F32), tail_shape, tail_shape, tail_shape]
    out_specs = [x_spec, tail_spec, tail_spec, tail_spec]
    scratch = []
    if prompt:
        scratch = [pltpu.VMEM((SUBLANES, D), F32)] * 3
    else:
        out_shape.append(jax.ShapeDtypeStruct(x.shape, F32))
        out_specs.append(x_spec)
    return pl.pallas_call(
        functools.partial(_mixer_kernel, prompt, R),
        grid=grid, in_specs=in_specs, out_specs=out_specs, out_shape=out_shape,
        scratch_shapes=scratch,
        compiler_params=pltpu.CompilerParams(dimension_semantics=sem, vmem_limit_bytes=VMEM_LIMIT),
        name="mixer_prompt" if prompt else "mixer_sample",
    )(*args)


def _ffn_kernel(prompt, final, R, x_ref, mod_ref, p_ref, w1_ref, w2_ref, gf_ref, o_ref):
    if prompt:
        m = mod_ref[pl.ds(pl.program_id(0), 1), :]
    else:
        m = _expand_rows(mod_ref[...], SUBLANES)
    shift2, scale2, gate2 = m[:, 0:D], m[:, D:2 * D], m[:, 2 * D:3 * D]
    x = x_ref[...]
    h2 = ((_rms(x) * p_ref[P_G2:P_G2 + 1, :]) * (1.0 + scale2) + shift2).astype(BF16)
    acts = []
    for c in range(D_FF // D):
        t = jnp.maximum(_wdot(h2, w1_ref[:, c * D:(c + 1) * D]), 0.0)
        acts.append((t * t).astype(BF16))
    x2 = x + gate2 * _wdot(jnp.concatenate(acts, axis=1), w2_ref[...])
    if final:
        x2 = _rms(x2) * gf_ref[...]
    o_ref[...] = x2


def _ffn_call(prompt, final, l, x, mod, w, g_final):
    if prompt:
        nbatch, t, _ = x.shape
        R = ROWS
        grid = (nbatch, t // R)
        x_spec = pl.BlockSpec((None, R, D), lambda b, j: (b, j, 0))
        mod_rows = mod.shape[1] - nbatch
        mod_spec = pl.BlockSpec((None, nbatch, 3 * D), lambda b, j: (l, mod_rows // nbatch, 1))
        sem = ("arbitrary", "arbitrary")
    else:
        R = ROWS
        grid = (x.shape[0] // R,)
        x_spec = pl.BlockSpec((R, D), lambda i: (i, 0))
        mod_spec = pl.BlockSpec((None, R // SUBLANES, 3 * D), lambda i: (l, i, 1))
        sem = ("arbitrary",)
    return pl.pallas_call(
        functools.partial(_ffn_kernel, prompt, final, R),
        grid=grid,
        in_specs=[x_spec, mod_spec, _resident(w["vec"], l), _resident(w["w_ff1"], l),
                  _resident(w["w_ff2"], l), pl.BlockSpec((1, D), lambda *_: (0, 0))],
        out_specs=x_spec,
        out_shape=jax.ShapeDtypeStruct(x.shape, F32),
        compiler_params=pltpu.CompilerParams(dimension_semantics=sem, vmem_limit_bytes=VMEM_LIMIT),
        name="ffn_prompt" if prompt else "ffn_sample",
    )(x, mod, w["vec"], w["w_ff1"], w["w_ff2"], g_final)


def _gate_weights(wa, wx):
    depth, heads, hd, _ = wa.shape
    per_tile = GATE_TILE // hd
    eye = jnp.eye(per_tile, dtype=wa.dtype)

    def blockdiag(wh):
        w5 = wh.reshape(depth, heads // per_tile, per_tile, hd, hd)
        out = w5[:, :, :, :, None, :] * eye[None, None, :, None, :, None]
        return out.reshape(depth, heads // per_tile, GATE_TILE, GATE_TILE)

    return jnp.concatenate([blockdiag(wa), blockdiag(wx)], axis=-1)


def _pack_rows(w):
    k, n = w.shape[-2:]
    pairs = w.astype(BF16).reshape(w.shape[:-2] + (k // 2, 2, n))
    return lax.bitcast_convert_type(jnp.swapaxes(pairs, -1, -2), jnp.uint32)


def _chunk_diagonals(cm_ws, steps):
    t = jnp.arange(steps)[None, :, None]
    d = jnp.arange(steps)[:, None, None]
    s = jnp.arange(steps)[None, None, :]
    on_diag = s == t - d
    corner = cm_ws[:, :, :steps, :steps]
    vals = jnp.sum(jnp.where(on_diag[None, None], corner[:, :, None], 0.0), axis=-1)
    vals = jnp.transpose(vals, (0, 2, 3, 1))
    return jnp.repeat(vals, GROUP_W, axis=-1)


def kernel(x_prompt, x_sample, c_prompt, c_sample, state_lru_h, state_lru_conv, state_sconv, w_ada, b_ada, g_norm1, g_norm2, g_final, w_in, lru_conv_w, lru_conv_b, lru_wa, lru_ba, lru_wx, lru_bx, lru_lambda, cm_ln_g, cm_ln_b, cm_ws, cm_bs, sc_conv_w, w_br_a, w_br_b, w_br_c, w_o, w_ff1, w_ff2):
    depth = w_in.shape[0]
    bp = x_prompt.shape[0]
    bs, ts, _ = x_sample.shape
    assert x_prompt.shape[1] % ROWS == 0 and (bs * ts) % ROWS == 0
    assert ts == SUBLANES and (bs * ts) % SAMPLE_MIXER_ROWS == 0 and bs % bp == 0

    mod = _modulation(jnp.concatenate([c_sample, c_prompt], axis=0), w_ada, b_ada)

    vec = jnp.concatenate(
        [p[:, None, :] for p in (g_norm1, g_norm2, lru_conv_b, lru_ba, lru_bx, lru_lambda,
                                  cm_ln_g, cm_ln_b)]
        + [lru_conv_w, sc_conv_w, jnp.zeros((depth, P_ROWS - P_SC_W - sc_conv_w.shape[1], D), F32)],
        axis=1)
    w = dict(vec=vec, w_in=_pack_rows(w_in), wg=_pack_rows(_gate_weights(lru_wa, lru_wx)),
             cm_ws=cm_ws,
             cm_bs=jnp.repeat(jnp.transpose(cm_bs, (0, 2, 1)), GROUP_W, axis=-1),
             cm_diag=_chunk_diagonals(cm_ws, ts),
             w_br_a=_pack_rows(w_br_a), w_br_b=_pack_rows(w_br_b), w_br_c=_pack_rows(w_br_c),
             w_o=_pack_rows(w_o), w_ff1=_pack_rows(w_ff1), w_ff2=_pack_rows(w_ff2))
    g_final2 = g_final.reshape(1, D)

    def pad_state(s):
        return jnp.pad(s, ((0, 0), (SUBLANES - s.shape[1], 0), (0, 0))).reshape(bs * SUBLANES, D)

    xp, xs = x_prompt, x_sample.reshape(bs * ts, D)
    prompt_tails, sample_tails, vs_l = [], [], []
    for l in range(depth):
        final = l == depth - 1
        xp, hp, lcp, scp = _mixer_call(True, l, xp, mod, None, w)
        xp = _ffn_call(True, final, l, xp, mod, w, g_final2)
        prompt_tails.append((hp, lcp, scp))

        state = (state_lru_h[l], pad_state(state_lru_conv[l]), pad_state(state_sconv[l]))
        xs, hs, lcs, scs, vs = _mixer_call(False, l, xs, mod, state, w)
        xs = _ffn_call(False, final, l, xs, mod, w, g_final2)
        sample_tails.append(tuple(t.reshape(bs, ts, D) for t in (hs, lcs, scs)))
        vs_l.append(vs.reshape(bs, ts, D))

    def assemble(tails):
        h = jnp.stack([t[0][:, SUBLANES - 1] for t in tails])
        lc = jnp.stack([t[1][:, SUBLANES - 3:] for t in tails])
        sc = jnp.stack([t[2][:, SUBLANES - 2:] for t in tails])
        return h, lc, sc

    return ((xp, xs.reshape(bs, ts, D)) + assemble(prompt_tails) + assemble(sample_tails)
            + (jnp.stack(vs_l),))
```
